```python
import jax, jax.numpy as jnp
from jax import lax
import numpy as np

D_MODEL = 1024
BATCH = 8
SEQ = 4096
DEPTH = 1

MEM_LEN = 256
EPS = 1e-6

LRU_WIDTH = D_MODEL
LRU_HEADS = 8
LRU_HEAD_DIM = LRU_WIDTH // LRU_HEADS
CONV_WIDTH = 4
LRU_C = 8.0

GLA_HEADS = 4
GLA_DV = D_MODEL // 8
GLA_DK = GLA_DV // 2
GLA_V_WIDTH = GLA_HEADS * GLA_DV
GLA_K_WIDTH = GLA_HEADS * GLA_DK
GLA_RANK = 16
GLA_TAU = 16.0
GLA_CHUNK = 64

XA_HEADS = 4
XA_HEAD_DIM = D_MODEL // 8
XA_WIDTH = XA_HEADS * XA_HEAD_DIM

MIX_WIDTH = LRU_WIDTH + GLA_V_WIDTH + XA_WIDTH
IN_SIZES = (LRU_WIDTH, GLA_K_WIDTH, GLA_K_WIDTH, GLA_V_WIDTH, GLA_RANK, XA_WIDTH, MIX_WIDTH)
IN_WIDTH = LRU_WIDTH + 2 * GLA_K_WIDTH + GLA_V_WIDTH + GLA_RANK + XA_WIDTH + MIX_WIDTH

kernel_name = "hybrid_rglru_gla_memxattn_parallel_heads"


def rms_norm(x, g):
    x32 = x.astype(jnp.float32)
    y = x32 * lax.rsqrt(jnp.mean(x32 * x32, axis=-1, keepdims=True) + EPS)
    return (y * g.astype(jnp.float32)).astype(x.dtype)


def causal_depthwise_conv(u, w, b):
    c = u.shape[-1]
    y = lax.conv_general_dilated(
        u, w[:, None, :].astype(u.dtype), window_strides=(1,),
        padding=[(CONV_WIDTH - 1, 0)], dimension_numbers=("NWC", "WIO", "NWC"),
        feature_group_count=c)
    return y + b.astype(u.dtype)


def rg_lru(u, w_a, b_a, w_i, b_i, lam):
    bsz, s, _ = u.shape
    uh = u.reshape(bsz, s, LRU_HEADS, LRU_HEAD_DIM)
    r = jax.nn.sigmoid((jnp.einsum("bshi,hij->bshj", uh, w_a) + b_a).astype(jnp.float32))
    i = jax.nn.sigmoid((jnp.einsum("bshi,hij->bshj", uh, w_i) + b_i).astype(jnp.float32))
    r = r.reshape(bsz, s, LRU_WIDTH)
    i = i.reshape(bsz, s, LRU_WIDTH)
    log_a = -LRU_C * r * jax.nn.softplus(-lam.astype(jnp.float32))
    a = jnp.exp(log_a)
    mult = jnp.sqrt(-jnp.expm1(2.0 * log_a))
    bx = mult * (i * u.astype(jnp.float32))

    def combine(left, right):
        a1, b1 = left
        a2, b2 = right
        return a1 * a2, a2 * b1 + b2

    _, h = lax.associative_scan(combine, (a, bx), axis=1)
    return h.astype(u.dtype)


def gla(q, k, v, g_lr, w_g2, b_g, norm_g):
    bsz, s, _ = q.shape
    n, c = s // GLA_CHUNK, GLA_CHUNK
    f32 = jnp.float32
    q = q.astype(f32).reshape(bsz, n, c, GLA_HEADS, GLA_DK) * (GLA_DK ** -0.5)
    k = k.astype(f32).reshape(bsz, n, c, GLA_HEADS, GLA_DK)
    v = v.astype(f32).reshape(bsz, n, c, GLA_HEADS, GLA_DV)
    logit = (g_lr @ w_g2 + b_g).astype(f32)
    log_alpha = jax.nn.log_sigmoid(logit) / GLA_TAU
    cum = jnp.cumsum(log_alpha.reshape(bsz, n, c, GLA_HEADS, GLA_DK), axis=2)
    cum_last = cum[:, :, -1:]
    q_dec = q * jnp.exp(cum)
    k_dec = k * jnp.exp(-cum)
    causal = jnp.tril(jnp.ones((c, c), dtype=bool))
    att = jnp.einsum("bnihk,bnjhk->bnhij", q_dec, k_dec)
    att = jnp.where(causal, att, 0.0)
    o_intra = jnp.einsum("bnhij,bnjhv->bnihv", att, v)
    kv = jnp.einsum("bnjhk,bnjhv->nbhkv", k * jnp.exp(cum_last - cum), v)
    decay = jnp.exp(cum_last[:, :, 0]).transpose(1, 0, 2, 3)

    def step(state, inp):
        d, kv_n = inp
        return d[..., None] * state + kv_n, state

    s0 = jnp.zeros((bsz, GLA_HEADS, GLA_DK, GLA_DV), f32)
    _, s_prev = lax.scan(step, s0, (decay, kv))
    o_inter = jnp.einsum("bnihk,nbhkv->bnihv", q_dec, s_prev)
    o = (o_intra + o_inter).reshape(bsz, s, GLA_HEADS, GLA_DV)
    o = o * lax.rsqrt(jnp.mean(o * o, axis=-1, keepdims=True) + EPS)
    o = o * norm_g.astype(f32).reshape(GLA_HEADS, GLA_DV)
    return o.reshape(bsz, s, GLA_V_WIDTH).astype(g_lr.dtype)


def memory_cross_attention(q, mem_n, w_mem_kv):
    bsz, s, _ = q.shape
    m = mem_n.shape[1]
    kvm = mem_n @ w_mem_kv
    km, vm = jnp.split(kvm, 2, axis=-1)
    qh = q.reshape(bsz, s, XA_HEADS, XA_HEAD_DIM)
    kh = km.reshape(bsz, m, XA_HEADS, XA_HEAD_DIM)
    vh = vm.reshape(bsz, m, XA_HEADS, XA_HEAD_DIM)
    scores = jnp.einsum("bshd,bmhd->bhsm", qh, kh).astype(jnp.float32) * (XA_HEAD_DIM ** -0.5)
    p = jax.nn.softmax(scores, axis=-1).astype(vh.dtype)
    o = jnp.einsum("bhsm,bmhd->bshd", p, vh)
    return o.reshape(bsz, s, XA_WIDTH)


def setup_inputs(seed: int = 0) -> dict:
    key = jax.random.key(seed)
    ks = jax.random.split(key, 20)
    f32 = jnp.float32
    nrm = lambda k, shape, scale: jax.random.normal(k, shape, f32) * scale
    x = jax.random.normal(ks[0], (BATCH, SEQ, D_MODEL), f32)
    mem = jax.random.normal(ks[1], (BATCH, MEM_LEN, D_MODEL), f32)
    norm_g = 1.0 + nrm(ks[2], (DEPTH, D_MODEL), 0.02)
    mem_norm_g = 1.0 + nrm(ks[3], (DEPTH, D_MODEL), 0.02)
    w_in = nrm(ks[4], (DEPTH, D_MODEL, IN_WIDTH), D_MODEL ** -0.5)
    conv_w = nrm(ks[5], (DEPTH, CONV_WIDTH, LRU_WIDTH), CONV_WIDTH ** -0.5)
    conv_b = nrm(ks[6], (DEPTH, LRU_WIDTH), 0.02)
    lru_w_a = nrm(ks[7], (DEPTH, LRU_HEADS, LRU_HEAD_DIM, LRU_HEAD_DIM), LRU_HEAD_DIM ** -0.5)
    lru_b_a = nrm(ks[8], (DEPTH, LRU_HEADS, LRU_HEAD_DIM), 0.02)
    lru_w_i = nrm(ks[9], (DEPTH, LRU_HEADS, LRU_HEAD_DIM, LRU_HEAD_DIM), LRU_HEAD_DIM ** -0.5)
    lru_b_i = nrm(ks[10], (DEPTH, LRU_HEADS, LRU_HEAD_DIM), 0.02)
    a_c = jax.random.uniform(ks[11], (DEPTH, LRU_WIDTH), f32, 0.9, 0.999)
    a0 = a_c ** (1.0 / LRU_C)
    lru_lambda = jnp.log(a0) - jnp.log1p(-a0)
    gla_w_g2 = nrm(ks[12], (DEPTH, GLA_RANK, GLA_K_WIDTH), GLA_RANK ** -0.5)
    gla_b_g = nrm(ks[13], (DEPTH, GLA_K_WIDTH), 0.02)
    gla_norm_g = 1.0 + nrm(ks[14], (DEPTH, GLA_V_WIDTH), 0.02)
    w_mem_kv = nrm(ks[15], (DEPTH, D_MODEL, 2 * XA_WIDTH), D_MODEL ** -0.5)
    w_out = nrm(ks[16], (DEPTH, MIX_WIDTH, D_MODEL), MIX_WIDTH ** -0.5)
    final_norm_g = 1.0 + nrm(ks[17], (D_MODEL,), 0.02)
    return {"x": x, "mem": mem, "norm_g": norm_g, "mem_norm_g": mem_norm_g, "w_in": w_in,
            "conv_w": conv_w, "conv_b": conv_b, "lru_w_a": lru_w_a, "lru_b_a": lru_b_a,
            "lru_w_i": lru_w_i, "lru_b_i": lru_b_i, "lru_lambda": lru_lambda,
            "gla_w_g2": gla_w_g2, "gla_b_g": gla_b_g, "gla_norm_g": gla_norm_g,
            "w_mem_kv": w_mem_kv, "w_out": w_out, "final_norm_g": final_norm_g}


def reference(x, mem, norm_g, mem_norm_g, w_in, conv_w, conv_b, lru_w_a, lru_b_a, lru_w_i,
              lru_b_i, lru_lambda, gla_w_g2, gla_b_g, gla_norm_g, w_mem_kv, w_out, final_norm_g):
    split_points = [int(v) for v in np.cumsum(IN_SIZES)[:-1]]
    for l in range(DEPTH):
        h = rms_norm(x, norm_g[l])
        proj = h @ w_in[l]
        u_lru, q_gla, k_gla, v_gla, g_lr, q_xa, gate = jnp.split(proj, split_points, axis=-1)
        u = causal_depthwise_conv(u_lru, conv_w[l], conv_b[l])
        y_lru = rg_lru(u, lru_w_a[l], lru_b_a[l], lru_w_i[l], lru_b_i[l], lru_lambda[l])
        y_gla = gla(q_gla, k_gla, v_gla, g_lr, gla_w_g2[l], gla_b_g[l], gla_norm_g[l])
        mem_n = rms_norm(mem, mem_norm_g[l])
        y_xa = memory_cross_attention(q_xa, mem_n, w_mem_kv[l])
        y = jnp.concatenate([y_lru, y_gla, y_xa], axis=-1) * jax.nn.silu(gate)
        x = x + y @ w_out[l]
    return rms_norm(x, final_norm_g)
```

```python
import jax
import jax.numpy as jnp
from jax import lax
from jax.experimental import pallas as pl
from jax.experimental.pallas import tpu as pltpu

F32 = jnp.float32
BF16 = jnp.bfloat16

D_MODEL = 1024
MEM_LEN = 256
EPS = 1e-6

LRU_WIDTH = 1024
LRU_HEADS = 8
LRU_HEAD_DIM = 128
CONV_WIDTH = 4
LRU_C = 8.0

GLA_HEADS = 4
GLA_DV = 128
GLA_DK = 64
GLA_V_WIDTH = GLA_HEADS * GLA_DV
GLA_K_WIDTH = GLA_HEADS * GLA_DK
GLA_RANK = 16
GLA_TAU = 16.0
GLA_CHUNK = 64

XA_HEADS = 4
XA_HEAD_DIM = 128
XA_WIDTH = XA_HEADS * XA_HEAD_DIM

MIX_WIDTH = LRU_WIDTH + GLA_V_WIDTH + XA_WIDTH

LANES = 128
SUBLANES = 8

GLR_PAD = LANES
P_Q = 0
P_K = P_Q + GLA_K_WIDTH
P_V = P_K + GLA_K_WIDTH
P_XQ = P_V + GLA_V_WIDTH
P_GATE = P_XQ + XA_WIDTH
P_GLR = P_GATE + MIX_WIDTH
P_END = P_GLR + GLR_PAD

Y_LRU = 0
Y_GLA = LRU_WIDTH
Y_XA = LRU_WIDTH + GLA_V_WIDTH

SEQ_TILE = 512
NORM_ROWS = 32
XA_ROWS = 128
VMEM_LIMIT_BYTES = 60 * 1024 * 1024


def _sigmoid(x):
    return 0.5 * jnp.tanh(0.5 * x) + 0.5


def _softplus(x):
    return jnp.maximum(x, 0.0) + jnp.log1p(jnp.exp(-jnp.abs(x)))


def _rms_scale(x):
    return lax.rsqrt(jnp.mean(x * x, axis=-1, keepdims=True) + EPS)


def _mem_kv_kernel(mem_ref, g_ref, w_ref, kt_ref, v_ref):
    m = mem_ref[...]
    mn = (m * _rms_scale(m) * g_ref[...]).astype(BF16)
    kv = jnp.dot(mn, w_ref[...], preferred_element_type=F32)
    kt_ref[...] = kv[:, :XA_WIDTH].T.astype(BF16)
    v_ref[...] = kv[:, XA_WIDTH:].astype(BF16)


def _layer_kernel(x_ref, w_lru_ref, w_rest_ref, w_ai_ref, b_a_ref, b_i_ref, conv_w_ref,
                  conv_b_ref, lam_ref, w_g2_ref, b_g_ref, gla_g_ref, kt_ref, vm_ref,
                  w_out_ref, norm_g_ref, fin_g_ref,
                  o_ref,
                  h_ref, upad_ref, proj_ref, ubf_ref, ri_ref, ymix_ref, ybf_ref,
                  tail_ref, hc_ref, st_ref):
    ts = x_ref.shape[0]
    t = pl.program_id(1)

    @pl.when(t == 0)
    def _reset_carries():
        upad_ref[0:SUBLANES, :] = jnp.zeros((SUBLANES, LRU_WIDTH), F32)
        hc_ref[...] = jnp.zeros(hc_ref.shape, F32)
        st_ref[...] = jnp.zeros(st_ref.shape, F32)

    norm_g = norm_g_ref[...]

    def norm_body(i, carry):
        r0 = pl.multiple_of(i * NORM_ROWS, NORM_ROWS)
        xx = x_ref[pl.ds(r0, NORM_ROWS), :]
        h_ref[pl.ds(r0, NORM_ROWS), :] = (xx * _rms_scale(xx) * norm_g).astype(BF16)
        return carry

    lax.fori_loop(0, ts // NORM_ROWS, norm_body, 0)

    upad_ref[SUBLANES:SUBLANES + ts, :] = jnp.dot(
        h_ref[...], w_lru_ref[...], preferred_element_type=F32)
    proj_ref[...] = jnp.dot(h_ref[...], w_rest_ref[...], preferred_element_type=F32)

    tail_ref[...] = upad_ref[ts:ts + SUBLANES, :]
    conv_w = conv_w_ref[...]
    conv_b = conv_b_ref[...]
    n_conv = ts // NORM_ROWS

    def conv_body(i, carry):
        r0 = pl.multiple_of((n_conv - 1 - i) * NORM_ROWS, NORM_ROWS)
        win = upad_ref[pl.ds(r0, NORM_ROWS + SUBLANES), :]
        acc = conv_b + conv_w[CONV_WIDTH - 1:CONV_WIDTH, :] * win[SUBLANES:, :]
        for j in range(1, CONV_WIDTH):
            acc = acc + (conv_w[CONV_WIDTH - 1 - j:CONV_WIDTH - j, :]
                         * win[SUBLANES - j:SUBLANES - j + NORM_ROWS, :])
        upad_ref[pl.ds(SUBLANES + r0, NORM_ROWS), :] = acc
        ubf_ref[pl.ds(r0, NORM_ROWS), :] = acc.astype(BF16)
        return carry

    lax.fori_loop(0, n_conv, conv_body, 0)
    upad_ref[0:SUBLANES, :] = tail_ref[...]

    for hd in range(LRU_HEADS):
        ri_ref[:, 2 * hd * LRU_HEAD_DIM:2 * (hd + 1) * LRU_HEAD_DIM] = jnp.dot(
            ubf_ref[:, hd * LRU_HEAD_DIM:(hd + 1) * LRU_HEAD_DIM], w_ai_ref[hd],
            preferred_element_type=F32)

    c_lam = -LRU_C * _softplus(-lam_ref[...])
    b_a = b_a_ref[...]
    b_i = b_i_ref[...]
    row8 = lax.broadcasted_iota(jnp.int32, (SUBLANES, LRU_HEAD_DIM), 0)

    def scan_body(i, hc):
        r0 = pl.multiple_of(i * SUBLANES, SUBLANES)
        ri = ri_ref[pl.ds(r0, SUBLANES), :]
        u = upad_ref[pl.ds(SUBLANES + r0, SUBLANES), :]
        outs = []
        carries = []
        for hd in range(LRU_HEADS):
            lo = hd * LRU_HEAD_DIM
            hi = lo + LRU_HEAD_DIM
            r = _sigmoid(ri[:, 2 * lo:2 * lo + LRU_HEAD_DIM] + b_a[:, lo:hi])
            ig = _sigmoid(ri[:, 2 * lo + LRU_HEAD_DIM:2 * hi] + b_i[:, lo:hi])
            log_a = r * c_lam[:, lo:hi]
            a = jnp.exp(log_a)
            one_minus_a2 = -(a * a + 1.0) * jnp.tanh(log_a)
            bx = jnp.sqrt(one_minus_a2) * (ig * u[:, lo:hi])
            for s in (1, 2, 4):
                keep = row8 >= s
                a_sh = pltpu.roll(a, s, 0)
                b_sh = pltpu.roll(bx, s, 0)
                bx = jnp.where(keep, a * b_sh + bx, bx)
                a = jnp.where(keep, a * a_sh, a)
            hh = a * hc[:, lo:hi] + bx
            outs.append(hh)
            carries.append(hh[SUBLANES - 1:SUBLANES, :])
        ymix_ref[pl.ds(r0, SUBLANES), Y_LRU:Y_LRU + LRU_WIDTH] = jnp.concatenate(outs, axis=1)
        return jnp.concatenate(carries, axis=1)

    hc_ref[...] = lax.fori_loop(0, ts // SUBLANES, scan_body, hc_ref[...])

    c = GLA_CHUNK
    kw = GLA_K_WIDTH
    vw = GLA_V_WIDTH
    w_g2 = w_g2_ref[...]
    b_g = b_g_ref[...]
    gla_g = gla_g_ref[...]
    ii = lax.broadcasted_iota(jnp.int32, (c, c), 0)
    jj = lax.broadcasted_iota(jnp.int32, (c, c), 1)
    tril = jnp.where(jj <= ii, 1.0, 0.0).astype(BF16)
    ci = lax.broadcasted_iota(jnp.int32, (c, kw), 0)
    cj = lax.broadcasted_iota(jnp.int32, (c, kw), 1)
    causal = (cj & (c - 1)) <= ci
    kr = lax.broadcasted_iota(jnp.int32, (GLA_HEADS * c, kw), 0)
    kc = lax.broadcasted_iota(jnp.int32, (GLA_HEADS * c, kw), 1)
    k_bd = (kr // c) == (kc // GLA_DK)
    vr = lax.broadcasted_iota(jnp.int32, (GLA_HEADS * c, vw), 0)
    vc = lax.broadcasted_iota(jnp.int32, (GLA_HEADS * c, vw), 1)
    v_bd = (vr // c) == (vc // GLA_DV)
    sr = lax.broadcasted_iota(jnp.int32, (vw, kw), 0)
    sc = lax.broadcasted_iota(jnp.int32, (vw, kw), 1)
    s_bd = (sr // GLA_DV) == (sc // GLA_DK)
    nt_dims = (((1,), (1,)), ((), ()))
    tn_dims = (((0,), (0,)), ((), ()))

    def gla_body(n, carry):
        r0 = pl.multiple_of(n * c, c)
        q = proj_ref[pl.ds(r0, c), P_Q:P_Q + kw]
        k = proj_ref[pl.ds(r0, c), P_K:P_K + kw]
        v = proj_ref[pl.ds(r0, c), P_V:P_V + vw].astype(BF16)
        glr = proj_ref[pl.ds(r0, c), P_GLR:P_GLR + GLR_PAD].astype(BF16)
        logit = jnp.dot(glr, w_g2, preferred_element_type=F32) + b_g
        log_alpha = -_softplus(-logit) * (1.0 / GLA_TAU)
        la_hi = log_alpha.astype(BF16)
        la_lo = (log_alpha - la_hi.astype(F32)).astype(BF16)
        cum = (jnp.dot(tril, la_hi, preferred_element_type=F32)
               + jnp.dot(tril, la_lo, preferred_element_type=F32))
        cum_last = cum[c - 1:c, :]
        q_dec = (q * (GLA_DK ** -0.5) * jnp.exp(cum)).astype(BF16)
        k_dec = (k * jnp.exp(-cum)).astype(BF16)
        k_rem = (k * jnp.exp(cum_last - cum)).astype(BF16)
        decay = jnp.exp(cum_last)
        k_rows = jnp.where(k_bd, jnp.concatenate([k_dec] * GLA_HEADS, axis=0), 0.0)
        att = lax.dot_general(q_dec, k_rows, nt_dims, preferred_element_type=F32)
        att = jnp.where(causal, att, 0.0).astype(BF16)
        v_rows = jnp.where(v_bd, jnp.concatenate([v] * GLA_HEADS, axis=0), 0.0)
        st = st_ref[...]
        o = (jnp.dot(att, v_rows, preferred_element_type=F32)
             + lax.dot_general(q_dec, st.astype(BF16), nt_dims, preferred_element_type=F32))
        kv_t = lax.dot_general(v, k_rem, tn_dims, preferred_element_type=F32)
        st_ref[...] = st * decay + jnp.where(s_bd, kv_t, 0.0)
        outs = []
        for hd in range(GLA_HEADS):
            oh = o[:, hd * GLA_DV:(hd + 1) * GLA_DV]
            outs.append(oh * _rms_scale(oh) * gla_g[:, hd * GLA_DV:(hd + 1) * GLA_DV])
        ymix_ref[pl.ds(r0, c), Y_GLA:Y_GLA + vw] = jnp.concatenate(outs, axis=1)
        return carry

    lax.fori_loop(0, ts // c, gla_body, 0)

    def xa_body(i, carry):
        r0 = pl.multiple_of(i * XA_ROWS, XA_ROWS)
        outs = []
        for hd in range(XA_HEADS):
            lo = hd * XA_HEAD_DIM
            qh = proj_ref[pl.ds(r0, XA_ROWS), P_XQ + lo:P_XQ + lo + XA_HEAD_DIM].astype(BF16)
            s = jnp.dot(qh, kt_ref[lo:lo + XA_HEAD_DIM, :],
                        preferred_element_type=F32) * (XA_HEAD_DIM ** -0.5)
            e = jnp.exp(s - jnp.max(s, axis=-1, keepdims=True))
            inv = 1.0 / jnp.sum(e, axis=-1, keepdims=True)
            pv = jnp.dot(e.astype(BF16), vm_ref[:, lo:lo + XA_HEAD_DIM],
                         preferred_element_type=F32)
            outs.append(pv * inv)
        ymix_ref[pl.ds(r0, XA_ROWS), Y_XA:Y_XA + XA_WIDTH] = jnp.concatenate(outs, axis=1)
        return carry

    lax.fori_loop(0, ts // XA_ROWS, xa_body, 0)

    def gate_body(i, carry):
        r0 = pl.multiple_of(i * NORM_ROWS, NORM_ROWS)
        hg = 0.5 * proj_ref[pl.ds(r0, NORM_ROWS), P_GATE:P_GATE + MIX_WIDTH]
        silu = hg + hg * jnp.tanh(hg)
        ybf_ref[pl.ds(r0, NORM_ROWS), :] = (ymix_ref[pl.ds(r0, NORM_ROWS), :] * silu).astype(BF16)
        return carry

    lax.fori_loop(0, ts // NORM_ROWS, gate_body, 0)

    o_ref[...] = jnp.dot(ybf_ref[...], w_out_ref[...], preferred_element_type=F32)
    fin_g = fin_g_ref[...]

    def out_body(i, carry):
        r0 = pl.multiple_of(i * NORM_ROWS, NORM_ROWS)
        z = x_ref[pl.ds(r0, NORM_ROWS), :] + o_ref[pl.ds(r0, NORM_ROWS), :]
        o_ref[pl.ds(r0, NORM_ROWS), :] = z * _rms_scale(z) * fin_g
        return carry

    lax.fori_loop(0, ts // NORM_ROWS, out_body, 0)


def _const_spec(shape):
    zeros = (0,) * len(shape)
    return pl.BlockSpec(shape, lambda b, t: zeros, pipeline_mode=pl.Buffered(1))


def kernel(x, mem, norm_g, mem_norm_g, w_in, conv_w, conv_b, lru_w_a, lru_b_a, lru_w_i, lru_b_i,
           lru_lambda, gla_w_g2, gla_b_g, gla_norm_g, w_mem_kv, w_out, final_norm_g):
    batch, seq, d = x.shape
    assert d == D_MODEL and w_in.shape[0] == 1 and seq % SEQ_TILE == 0
    ts = SEQ_TILE

    w = w_in[0]
    o_q = LRU_WIDTH
    o_k = o_q + GLA_K_WIDTH
    o_v = o_k + GLA_K_WIDTH
    o_glr = o_v + GLA_V_WIDTH
    o_xq = o_glr + GLA_RANK
    o_gate = o_xq + XA_WIDTH
    w_lru = w[:, :LRU_WIDTH].astype(BF16)
    w_glr = jnp.pad(w[:, o_glr:o_xq], ((0, 0), (0, GLR_PAD - GLA_RANK)))
    w_rest = jnp.concatenate(
        [w[:, o_q:o_glr], w[:, o_xq:o_gate], w[:, o_gate:], w_glr], axis=1).astype(BF16)
    w_ai = jnp.concatenate([lru_w_a[0], lru_w_i[0]], axis=-1).astype(BF16)
    w_g2 = jnp.pad(gla_w_g2[0], ((0, GLR_PAD - GLA_RANK), (0, 0))).astype(BF16)
    row = lambda a: a.reshape(1, -1).astype(F32)

    kt, vm = pl.pallas_call(
        _mem_kv_kernel,
        grid=(batch,),
        in_specs=[
            pl.BlockSpec((None, MEM_LEN, D_MODEL), lambda b: (b, 0, 0)),
            pl.BlockSpec((1, D_MODEL), lambda b: (0, 0)),
            pl.BlockSpec((D_MODEL, 2 * XA_WIDTH), lambda b: (0, 0)),
        ],
        out_specs=[
            pl.BlockSpec((None, XA_WIDTH, MEM_LEN), lambda b: (b, 0, 0)),
            pl.BlockSpec((None, MEM_LEN, XA_WIDTH), lambda b: (b, 0, 0)),
        ],
        out_shape=[
            jax.ShapeDtypeStruct((batch, XA_WIDTH, MEM_LEN), BF16),
            jax.ShapeDtypeStruct((batch, MEM_LEN, XA_WIDTH), BF16),
        ],
        name="mem_kv",
    )(mem, row(mem_norm_g[0]), w_mem_kv[0].astype(BF16))

    in_specs = [
        pl.BlockSpec((None, ts, D_MODEL), lambda b, t: (b, t, 0)),
        _const_spec((D_MODEL, LRU_WIDTH)),
        _const_spec((D_MODEL, P_END)),
        _const_spec((LRU_HEADS, LRU_HEAD_DIM, 2 * LRU_HEAD_DIM)),
        _const_spec((1, LRU_WIDTH)),
        _const_spec((1, LRU_WIDTH)),
        _const_spec((CONV_WIDTH, LRU_WIDTH)),
        _const_spec((1, LRU_WIDTH)),
        _const_spec((1, LRU_WIDTH)),
        _const_spec((GLR_PAD, GLA_K_WIDTH)),
        _const_spec((1, GLA_K_WIDTH)),
        _const_spec((1, GLA_V_WIDTH)),
        pl.BlockSpec((None, XA_WIDTH, MEM_LEN), lambda b, t: (b, 0, 0)),
        pl.BlockSpec((None, MEM_LEN, XA_WIDTH), lambda b, t: (b, 0, 0)),
        _const_spec((MIX_WIDTH, D_MODEL)),
        _const_spec((1, D_MODEL)),
        _const_spec((1, D_MODEL)),
    ]
    scratch_shapes = [
        pltpu.VMEM((ts, D_MODEL), BF16),
        pltpu.VMEM((ts + SUBLANES, LRU_WIDTH), F32),
        pltpu.VMEM((ts, P_END), F32),
        pltpu.VMEM((ts, LRU_WIDTH), BF16),
        pltpu.VMEM((ts, 2 * LRU_WIDTH), F32),
        pltpu.VMEM((ts, MIX_WIDTH), F32),
        pltpu.VMEM((ts, MIX_WIDTH), BF16),
        pltpu.VMEM((SUBLANES, LRU_WIDTH), F32),
        pltpu.VMEM((1, LRU_WIDTH), F32),
        pltpu.VMEM((GLA_V_WIDTH, GLA_K_WIDTH), F32),
    ]
    return pl.pallas_call(
        _layer_kernel,
        grid=(batch, seq // ts),
        in_specs=in_specs,
        out_specs=pl.BlockSpec((None, ts, D_MODEL), lambda b, t: (b, t, 0)),
        out_shape=jax.ShapeDtypeStruct((batch, seq, D_MODEL), F32),
        scratch_shapes=scratch_shapes,
        compiler_params=pltpu.CompilerParams(
            dimension_semantics=("arbitrary", "arbitrary"),
            vmem_limit_bytes=VMEM_LIMIT_BYTES),
        name="hybrid_layer",
    )(x, w_lru, w_rest, w_ai, row(lru_b_a[0]), row(lru_b_i[0]), conv_w[0].astype(F32),
      row(conv_b[0]), row(lru_lambda[0]), w_g2, row(gla_b_g[0]), row(gla_norm_g[0]),
      kt, vm, w_out[0].astype(BF16), row(norm_g[0]), row(final_norm_g))
```

```python
import jax
import jax.numpy as jnp
from jax import lax
from jax.experimental import pallas as pl
from jax.experimental.pallas import tpu as pltpu

F32 = jnp.float32
BF16 = jnp.bfloat16

D_MODEL = 1024
MEM_LEN = 256
EPS = 1e-6

LRU_WIDTH = 1024
LRU_HEADS = 8
LRU_HEAD_DIM = 128
CONV_WIDTH = 4
LRU_C = 8.0

GLA_HEADS = 4
GLA_DV = 128
GLA_DK = 64
GLA_V_WIDTH = GLA_HEADS * GLA_DV
GLA_K_WIDTH = GLA_HEADS * GLA_DK
GLA_RANK = 16
GLA_TAU = 16.0
GLA_CHUNK = 64

XA_HEADS = 4
XA_HEAD_DIM = 128
XA_WIDTH = XA_HEADS * XA_HEAD_DIM

MIX_WIDTH = LRU_WIDTH + GLA_V_WIDTH + XA_WIDTH

LANES = 128
SUBLANES = 8

GLR_PAD = LANES
P_Q = 0
P_K = P_Q + GLA_K_WIDTH
P_V = P_K + GLA_K_WIDTH
P_XQ = P_V + GLA_V_WIDTH
P_GATE = P_XQ + XA_WIDTH
P_GLR = P_GATE + MIX_WIDTH
P_END = P_GLR + GLR_PAD

Y_LRU = 0
Y_GLA = LRU_WIDTH
Y_XA = LRU_WIDTH + GLA_V_WIDTH

SEQ_TILE = 512
NORM_ROWS = 32
XA_ROWS = 128
VMEM_LIMIT_BYTES = 60 * 1024 * 1024


def _sigmoid(x):
    return 0.5 * jnp.tanh(0.5 * x) + 0.5


def _softplus(x):
    return jnp.maximum(x, 0.0) + jnp.log1p(jnp.exp(-jnp.abs(x)))


def _log_sigmoid(x):
    return jnp.minimum(x, 0.0) - jnp.log(1.0 + jnp.exp(-jnp.abs(x)))


def _rms_scale(x):
    return lax.rsqrt(jnp.mean(x * x, axis=-1, keepdims=True) + EPS)


def _mem_kv_kernel(mem_ref, g_ref, w_ref, kt_ref, v_ref):
    m = mem_ref[...]
    mn = (m * _rms_scale(m) * g_ref[...]).astype(BF16)
    kv = jnp.dot(mn, w_ref[...], preferred_element_type=F32)
    kt_ref[...] = kv[:, :XA_WIDTH].T.astype(BF16)
    v_ref[...] = kv[:, XA_WIDTH:].astype(BF16)


def _layer_kernel(x_ref, w_lru_ref, w_rest_ref, w_ai_ref, b_a_ref, b_i_ref, conv_w_ref,
                  conv_b_ref, lam_ref, w_g2_ref, b_g_ref, gla_g_ref, kt_ref, vm_ref,
                  w_out_ref, norm_g_ref, fin_g_ref,
                  o_ref,
                  h_ref, upad_ref, proj_ref, ubf_ref, ri_ref, ymix_ref, ybf_ref,
                  tail_ref, hc_ref, st_ref, clam_ref):
    ts = x_ref.shape[0]
    t = pl.program_id(1)

    @pl.when(t == 0)
    def _reset_carries():
        upad_ref[0:SUBLANES, :] = jnp.zeros((SUBLANES, LRU_WIDTH), F32)
        hc_ref[...] = jnp.zeros(hc_ref.shape, F32)
        st_ref[...] = jnp.zeros(st_ref.shape, F32)

    norm_g = norm_g_ref[...]

    def norm_body(i, carry):
        r0 = pl.multiple_of(i * NORM_ROWS, NORM_ROWS)
        xx = x_ref[pl.ds(r0, NORM_ROWS), :]
        h_ref[pl.ds(r0, NORM_ROWS), :] = (xx * _rms_scale(xx) * norm_g).astype(BF16)
        return carry

    lax.fori_loop(0, ts // NORM_ROWS, norm_body, 0, unroll=4)

    upad_ref[SUBLANES:SUBLANES + ts, :] = jnp.dot(
        h_ref[...], w_lru_ref[...], preferred_element_type=F32)
    proj_ref[...] = jnp.dot(h_ref[...], w_rest_ref[...], preferred_element_type=F32)

    tail_ref[...] = upad_ref[ts:ts + SUBLANES, :]
    conv_w = conv_w_ref[...]
    conv_b = conv_b_ref[...]
    n_conv = ts // NORM_ROWS

    def conv_body(i, carry):
        r0 = pl.multiple_of((n_conv - 1 - i) * NORM_ROWS, NORM_ROWS)
        win = upad_ref[pl.ds(r0, NORM_ROWS + SUBLANES), :]
        acc = conv_b + conv_w[CONV_WIDTH - 1:CONV_WIDTH, :] * win[SUBLANES:, :]
        for j in range(1, CONV_WIDTH):
            acc = acc + (conv_w[CONV_WIDTH - 1 - j:CONV_WIDTH - j, :]
                         * win[SUBLANES - j:SUBLANES - j + NORM_ROWS, :])
        upad_ref[pl.ds(SUBLANES + r0, NORM_ROWS), :] = acc
        ubf_ref[pl.ds(r0, NORM_ROWS), :] = acc.astype(BF16)
        return carry

    lax.fori_loop(0, n_conv, conv_body, 0)
    upad_ref[0:SUBLANES, :] = tail_ref[...]

    for hd in range(LRU_HEADS):
        ri_ref[:, 2 * hd * LRU_HEAD_DIM:2 * (hd + 1) * LRU_HEAD_DIM] = jnp.dot(
            ubf_ref[:, hd * LRU_HEAD_DIM:(hd + 1) * LRU_HEAD_DIM], w_ai_ref[hd],
            preferred_element_type=F32)

    clam_ref[...] = -LRU_C * _softplus(-lam_ref[...])
    row8 = lax.broadcasted_iota(jnp.int32, (SUBLANES, LRU_HEAD_DIM), 0)

    def scan_body(i, hc):
        r0 = pl.multiple_of(i * SUBLANES, SUBLANES)
        outs = []
        carries = []
        for hd in range(LRU_HEADS):
            lo = hd * LRU_HEAD_DIM
            hi = lo + LRU_HEAD_DIM
            r_pre = ri_ref[pl.ds(r0, SUBLANES), 2 * lo:2 * lo + LRU_HEAD_DIM]
            i_pre = ri_ref[pl.ds(r0, SUBLANES), 2 * lo + LRU_HEAD_DIM:2 * hi]
            u = upad_ref[pl.ds(SUBLANES + r0, SUBLANES), lo:hi]
            r = _sigmoid(r_pre + b_a_ref[:, lo:hi])
            ig = _sigmoid(i_pre + b_i_ref[:, lo:hi])
            log_a = r * clam_ref[:, lo:hi]
            a = jnp.exp(log_a)
            one_minus_a2 = -(a * a + 1.0) * jnp.tanh(log_a)
            bx = jnp.sqrt(one_minus_a2) * (ig * u)
            for s in (1, 2, 4):
                keep = row8 >= s
                a_sh = pltpu.roll(a, s, 0)
                b_sh = pltpu.roll(bx, s, 0)
                bx = jnp.where(keep, a * b_sh + bx, bx)
                a = jnp.where(keep, a * a_sh, a)
            hh = a * hc[:, lo:hi] + bx
            outs.append(hh)
            carries.append(hh[SUBLANES - 1:SUBLANES, :])
        ymix_ref[pl.ds(r0, SUBLANES), Y_LRU:Y_LRU + LRU_WIDTH] = jnp.concatenate(outs, axis=1)
        return jnp.concatenate(carries, axis=1)

    hc_ref[...] = lax.fori_loop(0, ts // SUBLANES, scan_body, hc_ref[...])

    c = GLA_CHUNK
    kw = GLA_K_WIDTH
    vw = GLA_V_WIDTH
    w_g2 = w_g2_ref[...]
    b_g = b_g_ref[...]
    gla_g = gla_g_ref[...]
    ii = lax.broadcasted_iota(jnp.int32, (c, c), 0)
    jj = lax.broadcasted_iota(jnp.int32, (c, c), 1)
    tril = jnp.where(jj <= ii, 1.0, 0.0).astype(BF16)
    ci = lax.broadcasted_iota(jnp.int32, (c, kw), 0)
    cj = lax.broadcasted_iota(jnp.int32, (c, kw), 1)
    causal = (cj & (c - 1)) <= ci
    kr = lax.broadcasted_iota(jnp.int32, (GLA_HEADS * c, kw), 0)
    kc = lax.broadcasted_iota(jnp.int32, (GLA_HEADS * c, kw), 1)
    k_bd = (kr // c) == (kc // GLA_DK)
    vr = lax.broadcasted_iota(jnp.int32, (GLA_HEADS * c, vw), 0)
    vc = lax.broadcasted_iota(jnp.int32, (GLA_HEADS * c, vw), 1)
    v_bd = (vr // c) == (vc // GLA_DV)
    sr = lax.broadcasted_iota(jnp.int32, (vw, kw), 0)
    sc = lax.broadcasted_iota(jnp.int32, (vw, kw), 1)
    s_bd = (sr // GLA_DV) == (sc // GLA_DK)
    nt_dims = (((1,), (1,)), ((), ()))
    tn_dims = (((0,), (0,)), ((), ()))

    def gla_body(n, carry):
        r0 = pl.multiple_of(n * c, c)
        q = proj_ref[pl.ds(r0, c), P_Q:P_Q + kw]
        k = proj_ref[pl.ds(r0, c), P_K:P_K + kw]
        v = proj_ref[pl.ds(r0, c), P_V:P_V + vw].astype(BF16)
        glr = proj_ref[pl.ds(r0, c), P_GLR:P_GLR + GLR_PAD].astype(BF16)
        logit = jnp.dot(glr, w_g2, preferred_element_type=F32) + b_g
        log_alpha = _log_sigmoid(logit) * (1.0 / GLA_TAU)
        la_hi = log_alpha.astype(BF16)
        la_lo = (log_alpha - la_hi.astype(F32)).astype(BF16)
        cum = (jnp.dot(tril, la_hi, preferred_element_type=F32)
               + jnp.dot(tril, la_lo, preferred_element_type=F32))
        cum_last = cum[c - 1:c, :]
        q_dec = (q * (GLA_DK ** -0.5) * jnp.exp(cum)).astype(BF16)
        k_dec = (k * jnp.exp(-cum)).astype(BF16)
        k_rem = (k * jnp.exp(cum_last - cum)).astype(BF16)
        decay = jnp.exp(cum_last)
        k_rows = jnp.where(k_bd, jnp.concatenate([k_dec] * GLA_HEADS, axis=0), 0.0)
        att = lax.dot_general(q_dec, k_rows, nt_dims, preferred_element_type=F32)
        att = jnp.where(causal, att, 0.0).astype(BF16)
        v_rows = jnp.where(v_bd, jnp.concatenate([v] * GLA_HEADS, axis=0), 0.0)
        st = st_ref[...]
        o = (jnp.dot(att, v_rows, preferred_element_type=F32)
             + lax.dot_general(q_dec, st.astype(BF16), nt_dims, preferred_element_type=F32))
        kv_t = lax.dot_general(v, k_rem, tn_dims, preferred_element_type=F32)
        st_ref[...] = st * decay + jnp.where(s_bd, kv_t, 0.0)
        outs = []
        for hd in range(GLA_HEADS):
            oh = o[:, hd * GLA_DV:(hd + 1) * GLA_DV]
            outs.append(oh * _rms_scale(oh) * gla_g[:, hd * GLA_DV:(hd + 1) * GLA_DV])
        ymix_ref[pl.ds(r0, c), Y_GLA:Y_GLA + vw] = jnp.concatenate(outs, axis=1)
        return carry

    lax.fori_loop(0, ts // c, gla_body, 0, unroll=True)

    def xa_body(i, carry):
        r0 = pl.multiple_of(i * XA_ROWS, XA_ROWS)
        outs = []
        for hd in range(XA_HEADS):
            lo = hd * XA_HEAD_DIM
            qh = proj_ref[pl.ds(r0, XA_ROWS), P_XQ + lo:P_XQ + lo + XA_HEAD_DIM].astype(BF16)
            s = jnp.dot(qh, kt_ref[lo:lo + XA_HEAD_DIM, :],
                        preferred_element_type=F32) * (XA_HEAD_DIM ** -0.5)
            e = jnp.exp(s - jnp.max(s, axis=-1, keepdims=True))
            inv = 1.0 / jnp.sum(e, axis=-1, keepdims=True)
            pv = jnp.dot(e.astype(BF16), vm_ref[:, lo:lo + XA_HEAD_DIM],
                         preferred_element_type=F32)
            outs.append(pv * inv)
        ymix_ref[pl.ds(r0, XA_ROWS), Y_XA:Y_XA + XA_WIDTH] = jnp.concatenate(outs, axis=1)
        return carry

    lax.fori_loop(0, ts // XA_ROWS, xa_body, 0, unroll=True)

    def gate_body(i, carry):
        r0 = pl.multiple_of(i * NORM_ROWS, NORM_ROWS)
        hg = 0.5 * proj_ref[pl.ds(r0, NORM_ROWS), P_GATE:P_GATE + MIX_WIDTH]
        silu = hg + hg * jnp.tanh(hg)
        ybf_ref[pl.ds(r0, NORM_ROWS), :] = (ymix_ref[pl.ds(r0, NORM_ROWS), :] * silu).astype(BF16)
        return carry

    lax.fori_loop(0, ts // NORM_ROWS, gate_body, 0)

    proj_ref[:, 0:D_MODEL] = jnp.dot(ybf_ref[...], w_out_ref[...], preferred_element_type=F32)
    fin_g = fin_g_ref[...]

    def out_body(i, carry):
        r0 = pl.multiple_of(i * NORM_ROWS, NORM_ROWS)
        z = x_ref[pl.ds(r0, NORM_ROWS), :] + proj_ref[pl.ds(r0, NORM_ROWS), 0:D_MODEL]
        o_ref[pl.ds(r0, NORM_ROWS), :] = z * _rms_scale(z) * fin_g
        return carry

    lax.fori_loop(0, ts // NORM_ROWS, out_body, 0, unroll=4)


def _const_spec(shape):
    zeros = (0,) * len(shape)
    return pl.BlockSpec(shape, lambda b, t: zeros, pipeline_mode=pl.Buffered(1))


def kernel(x, mem, norm_g, mem_norm_g, w_in, conv_w, conv_b, lru_w_a, lru_b_a, lru_w_i, lru_b_i,
           lru_lambda, gla_w_g2, gla_b_g, gla_norm_g, w_mem_kv, w_out, final_norm_g):
    batch, seq, d = x.shape
    assert d == D_MODEL and w_in.shape[0] == 1 and seq % SEQ_TILE == 0
    ts = SEQ_TILE

    w = w_in[0]
    o_q = LRU_WIDTH
    o_k = o_q + GLA_K_WIDTH
    o_v = o_k + GLA_K_WIDTH
    o_glr = o_v + GLA_V_WIDTH
    o_xq = o_glr + GLA_RANK
    o_gate = o_xq + XA_WIDTH
    w_lru = w[:, :LRU_WIDTH].astype(BF16)
    w_glr = jnp.pad(w[:, o_glr:o_xq], ((0, 0), (0, GLR_PAD - GLA_RANK)))
    w_rest = jnp.concatenate(
        [w[:, o_q:o_glr], w[:, o_xq:o_gate], w[:, o_gate:], w_glr], axis=1).astype(BF16)
    w_ai = jnp.concatenate([lru_w_a[0], lru_w_i[0]], axis=-1).astype(BF16)
    w_g2 = jnp.pad(gla_w_g2[0], ((0, GLR_PAD - GLA_RANK), (0, 0))).astype(BF16)
    row = lambda a: a.reshape(1, -1).astype(F32)
    rows8 = lambda a: jnp.broadcast_to(row(a), (SUBLANES, a.size))

    kt, vm = pl.pallas_call(
        _mem_kv_kernel,
        grid=(batch,),
        in_specs=[
            pl.BlockSpec((None, MEM_LEN, D_MODEL), lambda b: (b, 0, 0)),
            pl.BlockSpec((1, D_MODEL), lambda b: (0, 0)),
            pl.BlockSpec((D_MODEL, 2 * XA_WIDTH), lambda b: (0, 0)),
        ],
        out_specs=[
            pl.BlockSpec((None, XA_WIDTH, MEM_LEN), lambda b: (b, 0, 0)),
            pl.BlockSpec((None, MEM_LEN, XA_WIDTH), lambda b: (b, 0, 0)),
        ],
        out_shape=[
            jax.ShapeDtypeStruct((batch, XA_WIDTH, MEM_LEN), BF16),
            jax.ShapeDtypeStruct((batch, MEM_LEN, XA_WIDTH), BF16),
        ],
        name="mem_kv",
    )(mem, row(mem_norm_g[0]), w_mem_kv[0].astype(BF16))

    in_specs = [
        pl.BlockSpec((None, ts, D_MODEL), lambda b, t: (b, t, 0)),
        _const_spec((D_MODEL, LRU_WIDTH)),
        _const_spec((D_MODEL, P_END)),
        _const_spec((LRU_HEADS, LRU_HEAD_DIM, 2 * LRU_HEAD_DIM)),
        _const_spec((SUBLANES, LRU_WIDTH)),
        _const_spec((SUBLANES, LRU_WIDTH)),
        _const_spec((CONV_WIDTH, LRU_WIDTH)),
        _const_spec((1, LRU_WIDTH)),
        _const_spec((SUBLANES, LRU_WIDTH)),
        _const_spec((GLR_PAD, GLA_K_WIDTH)),
        _const_spec((1, GLA_K_WIDTH)),
        _const_spec((1, GLA_V_WIDTH)),
        pl.BlockSpec((None, XA_WIDTH, MEM_LEN), lambda b, t: (b, 0, 0)),
        pl.BlockSpec((None, MEM_LEN, XA_WIDTH), lambda b, t: (b, 0, 0)),
        _const_spec((MIX_WIDTH, D_MODEL)),
        _const_spec((1, D_MODEL)),
        _const_spec((1, D_MODEL)),
    ]
    scratch_shapes = [
        pltpu.VMEM((ts, D_MODEL), BF16),
        pltpu.VMEM((ts + SUBLANES, LRU_WIDTH), F32),
        pltpu.VMEM((ts, P_END), F32),
        pltpu.VMEM((ts, LRU_WIDTH), BF16),
        pltpu.VMEM((ts, 2 * LRU_WIDTH), F32),
        pltpu.VMEM((ts, MIX_WIDTH), F32),
        pltpu.VMEM((ts, MIX_WIDTH), BF16),
        pltpu.VMEM((SUBLANES, LRU_WIDTH), F32),
        pltpu.VMEM((1, LRU_WIDTH), F32),
        pltpu.VMEM((GLA_V_WIDTH, GLA_K_WIDTH), F32),
        pltpu.VMEM((SUBLANES, LRU_WIDTH), F32),
    ]
    return pl.pallas_call(
        _layer_kernel,
        grid=(batch, seq // ts),
        in_specs=in_specs,
        out_specs=pl.BlockSpec((None, ts, D_MODEL), lambda b, t: (b, t, 0)),
        out_shape=jax.ShapeDtypeStruct((batch, seq, D_MODEL), F32),
        scratch_shapes=scratch_shapes,
        compiler_params=pltpu.CompilerParams(
            dimension_semantics=("arbitrary", "arbitrary"),
            vmem_limit_bytes=VMEM_LIMIT_BYTES),
        name="hybrid_layer",
    )(x, w_lru, w_rest, w_ai, rows8(lru_b_a[0]), rows8(lru_b_i[0]), conv_w[0].astype(F32),
      row(conv_b[0]), rows8(lru_lambda[0]), w_g2, row(gla_b_g[0]), row(gla_norm_g[0]),
      kt, vm, w_out[0].astype(BF16), row(norm_g[0]), row(final_norm_g))
```
